```python
import math
import jax, jax.numpy as jnp
from jax import lax
import numpy as np

D_MODEL = 2048
BATCH = 16
SEQ = 2048
DEPTH = 4

N_MIXERS = 3
N_CONV_LAYERS = (DEPTH + 2) // 3
N_MLA_LAYERS = (DEPTH + 1) // 3
N_SSM_LAYERS = DEPTH // 3
DEEPNORM_ALPHA = (2 * DEPTH) ** 0.25
DEEPNORM_BETA = (8 * DEPTH) ** -0.25
LN_EPS = 1e-5
RMS_EPS = 1e-6

SC_WIDTH = 3

MLA_HEADS = 16
MLA_Q_LORA = 512
MLA_KV_LORA = 512
MLA_NOPE = 128
MLA_ROPE = 64
MLA_V = 128
MLA_QK = MLA_NOPE + MLA_ROPE
MLA_Q_BLOCK = 128
ROPE_THETA = 10000.0

SSM_INNER = 2 * D_MODEL
SSM_HEAD_DIM = 64
SSM_HEADS = SSM_INNER // SSM_HEAD_DIM
SSM_GROUPS = 8
SSM_HEADS_PER_GROUP = SSM_HEADS // SSM_GROUPS
SSM_STATE = 128
SSM_CONV = 4
SSM_CHUNK = 128
SSM_CONV_DIM = SSM_INNER + 2 * SSM_GROUPS * SSM_STATE
SSM_IN_DIM = SSM_INNER + SSM_CONV_DIM + SSM_HEADS

PEER_HEADS = 8
PEER_KEY_DIM = 256
PEER_HALF = PEER_KEY_DIM // 2
PEER_N_KEYS = 128
PEER_EXPERTS = PEER_N_KEYS * PEER_N_KEYS
PEER_TOPK = 16
PEER_TOKEN_BLOCK = 128

kernel_name = 'hybrid_conv_mla_ssd_peer_deepnorm'


def layer_norm(x, g, b):
    xf = x.astype(jnp.float32)
    mu = jnp.mean(xf, axis=-1, keepdims=True)
    var = jnp.mean(jnp.square(xf - mu), axis=-1, keepdims=True)
    y = (xf - mu) * lax.rsqrt(var + LN_EPS) * g.astype(jnp.float32) + b.astype(jnp.float32)
    return y.astype(x.dtype)


def rms_norm(x, w):
    xf = x.astype(jnp.float32)
    y = xf * lax.rsqrt(jnp.mean(jnp.square(xf), axis=-1, keepdims=True) + RMS_EPS)
    return (y * w.astype(jnp.float32)).astype(x.dtype)


def causal_dwconv(u, w):
    k, c = w.shape
    return lax.conv_general_dilated(
        u, w[:, None, :].astype(u.dtype), window_strides=(1,), padding=[(k - 1, 0)],
        dimension_numbers=('NWC', 'WIO', 'NWC'), feature_group_count=c)


def rotate_half(x):
    x1, x2 = jnp.split(x, 2, axis=-1)
    return jnp.concatenate([-x2, x1], axis=-1)


def short_conv_mixer(x, w_in, conv_w, w_out):
    proj = x @ w_in
    gate_b, gate_c, h = jnp.split(proj, 3, axis=-1)
    y = gate_b * causal_dwconv(gate_c * h, conv_w)
    return y @ w_out


def mla_mixer(x, positions, w_in, q_norm, kv_norm, w_uq, w_ukv, w_o):
    bsz, s, _ = x.shape
    proj = x @ w_in
    c_q = rms_norm(proj[..., :MLA_Q_LORA], q_norm)
    c_kv = rms_norm(proj[..., MLA_Q_LORA:MLA_Q_LORA + MLA_KV_LORA], kv_norm)
    k_rope = proj[..., MLA_Q_LORA + MLA_KV_LORA:]
    q = (c_q @ w_uq).reshape(bsz, s, MLA_HEADS, MLA_QK)
    q_nope, q_rope = q[..., :MLA_NOPE], q[..., MLA_NOPE:]
    kv = (c_kv @ w_ukv).reshape(bsz, s, MLA_HEADS, MLA_NOPE + MLA_V)
    k_nope, v = kv[..., :MLA_NOPE], kv[..., MLA_NOPE:]
    inv_freq = 1.0 / (ROPE_THETA ** (jnp.arange(0, MLA_ROPE, 2, dtype=jnp.float32) / MLA_ROPE))
    ang = positions.astype(jnp.float32)[..., None] * inv_freq
    ang = jnp.concatenate([ang, ang], axis=-1)
    cos, sin = jnp.cos(ang), jnp.sin(ang)
    q_rope = (q_rope.astype(jnp.float32) * cos[:, :, None] + rotate_half(q_rope.astype(jnp.float32)) * sin[:, :, None]).astype(x.dtype)
    k_rope = (k_rope.astype(jnp.float32) * cos + rotate_half(k_rope.astype(jnp.float32)) * sin).astype(x.dtype)
    nb = s // MLA_Q_BLOCK
    qn_blocks = jnp.moveaxis(q_nope.reshape(bsz, nb, MLA_Q_BLOCK, MLA_HEADS, MLA_NOPE), 1, 0)
    qr_blocks = jnp.moveaxis(q_rope.reshape(bsz, nb, MLA_Q_BLOCK, MLA_HEADS, MLA_ROPE), 1, 0)
    q_idx = jnp.arange(s).reshape(nb, MLA_Q_BLOCK)
    k_idx = jnp.arange(s)
    scale = 1.0 / math.sqrt(MLA_QK)

    def attend(args):
        qn, qr, qi = args
        sc = (jnp.einsum('bqhd,bkhd->bhqk', qn, k_nope)
              + jnp.einsum('bqhd,bkd->bhqk', qr, k_rope)).astype(jnp.float32) * scale
        sc = jnp.where((qi[:, None] >= k_idx[None, :])[None, None], sc, -jnp.inf)
        p = jax.nn.softmax(sc, axis=-1).astype(v.dtype)
        return jnp.einsum('bhqk,bkhd->bqhd', p, v)

    o = lax.map(attend, (qn_blocks, qr_blocks, q_idx))
    o = jnp.moveaxis(o, 0, 1).reshape(bsz, s, MLA_HEADS * MLA_V)
    return o @ w_o


def ssd_chunked_scan(xs, dt, a, bm, cm):
    bsz, s, g, e, p = xs.shape
    n = bm.shape[-1]
    nc = s // SSM_CHUNK
    da = dt * a
    xdt = xs * dt[..., None]
    to_chunks = lambda t: jnp.moveaxis(t.reshape((bsz, nc, SSM_CHUNK) + t.shape[2:]), 1, 0)
    causal = jnp.tril(jnp.ones((SSM_CHUNK, SSM_CHUNK), dtype=bool))

    def step(h, inp):
        xdt_c, da_c, b_c, c_c = inp
        acum = jnp.cumsum(da_c, axis=1)
        a_t = jnp.transpose(acum, (0, 2, 3, 1))
        seg = a_t[..., :, None] - a_t[..., None, :]
        decay = jnp.exp(jnp.where(causal, seg, -jnp.inf))
        cb = jnp.einsum('btgn,bsgn->bgts', c_c, b_c)
        y_diag = jnp.einsum('bgets,bsgep->btgep', cb[:, :, None] * decay, xdt_c)
        y_off = jnp.einsum('btgn,bgepn->btgep', c_c, h) * jnp.exp(acum)[..., None]
        decay_last = jnp.exp(a_t[..., -1:] - a_t)
        xdt_dec = xdt_c * jnp.transpose(decay_last, (0, 3, 1, 2))[..., None]
        h_new = h * jnp.exp(a_t[..., -1])[..., None, None] + jnp.einsum('bsgn,bsgep->bgepn', b_c, xdt_dec)
        return h_new, y_diag + y_off

    h0 = jnp.zeros((bsz, g, e, p, n), dtype=jnp.float32)
    _, ys = lax.scan(step, h0, (to_chunks(xdt), to_chunks(da), to_chunks(bm), to_chunks(cm)))
    return jnp.moveaxis(ys, 0, 1).reshape(bsz, s, g, e, p)


def mamba2_mixer(x, w_in, conv_w, conv_b, dt_bias, a_log, d_skip, norm_w, w_out):
    bsz, s, _ = x.shape
    proj = x @ w_in
    z = proj[..., :SSM_INNER]
    xbc = proj[..., SSM_INNER:SSM_INNER + SSM_CONV_DIM]
    dt_raw = proj[..., SSM_INNER + SSM_CONV_DIM:]
    xbc = jax.nn.silu(causal_dwconv(xbc, conv_w) + conv_b.astype(xbc.dtype))
    gbc = SSM_GROUPS * SSM_STATE
    xs = xbc[..., :SSM_INNER].astype(jnp.float32).reshape(bsz, s, SSM_GROUPS, SSM_HEADS_PER_GROUP, SSM_HEAD_DIM)
    bm = xbc[..., SSM_INNER:SSM_INNER + gbc].astype(jnp.float32).reshape(bsz, s, SSM_GROUPS, SSM_STATE)
    cm = xbc[..., SSM_INNER + gbc:].astype(jnp.float32).reshape(bsz, s, SSM_GROUPS, SSM_STATE)
    dt = jax.nn.softplus(dt_raw.astype(jnp.float32) + dt_bias.astype(jnp.float32))
    dt = dt.reshape(bsz, s, SSM_GROUPS, SSM_HEADS_PER_GROUP)
    a = -jnp.exp(a_log.astype(jnp.float32)).reshape(SSM_GROUPS, SSM_HEADS_PER_GROUP)
    y = ssd_chunked_scan(xs, dt, a, bm, cm)
    y = y + d_skip.astype(jnp.float32).reshape(SSM_GROUPS, SSM_HEADS_PER_GROUP)[:, :, None] * xs
    y = y.reshape(bsz, s, SSM_INNER) * jax.nn.silu(z.astype(jnp.float32))
    yg = y.reshape(bsz, s, SSM_GROUPS, SSM_INNER // SSM_GROUPS)
    yg = yg * lax.rsqrt(jnp.mean(jnp.square(yg), axis=-1, keepdims=True) + RMS_EPS)
    y = (yg.reshape(bsz, s, SSM_INNER) * norm_w.astype(jnp.float32)).astype(x.dtype)
    return y @ w_out


def peer_ffn(x, w_q, sub_keys, u_tab, v_tab):
    bsz, s, d = x.shape
    t = bsz * s
    xt = x.reshape(t, d)
    q = (xt @ w_q).reshape(t, PEER_HEADS, PEER_KEY_DIM)
    s1 = jnp.einsum('thd,hkd->thk', q[..., :PEER_HALF], sub_keys[:, 0]).astype(jnp.float32)
    s2 = jnp.einsum('thd,hkd->thk', q[..., PEER_HALF:], sub_keys[:, 1]).astype(jnp.float32)
    v1, i1 = lax.top_k(s1, PEER_TOPK)
    v2, i2 = lax.top_k(s2, PEER_TOPK)
    cand = (v1[..., :, None] + v2[..., None, :]).reshape(t, PEER_HEADS, PEER_TOPK * PEER_TOPK)
    cidx = (i1[..., :, None] * PEER_N_KEYS + i2[..., None, :]).reshape(t, PEER_HEADS, PEER_TOPK * PEER_TOPK)
    top_s, pos = lax.top_k(cand, PEER_TOPK)
    idx = jnp.take_along_axis(cidx, pos, axis=-1)
    gate = jax.nn.softmax(top_s, axis=-1)
    nblk = t // PEER_TOKEN_BLOCK
    kk = PEER_HEADS * PEER_TOPK
    xb = xt.reshape(nblk, PEER_TOKEN_BLOCK, d)
    ib = idx.reshape(nblk, PEER_TOKEN_BLOCK, kk)
    gb = gate.reshape(nblk, PEER_TOKEN_BLOCK, kk)

    def experts(args):
        xc, ic, gc = args
        u = jnp.take(u_tab, ic, axis=0)
        hid = jax.nn.gelu(jnp.einsum('td,tkd->tk', xc, u).astype(jnp.float32), approximate=False) * gc
        vv = jnp.take(v_tab, ic, axis=0)
        return jnp.einsum('tk,tkd->td', hid.astype(vv.dtype), vv)

    out = lax.map(experts, (xb, ib, gb))
    return out.reshape(bsz, s, d)


def setup_inputs(seed: int = 0) -> dict:
    key = jax.random.key(seed)
    ks = jax.random.split(key, 32)
    nrm = lambda k, shape, sc: jax.random.normal(k, shape, dtype=jnp.float32) * sc
    d = D_MODEL
    x = nrm(ks[0], (BATCH, SEQ, d), 1.0)
    positions = jnp.broadcast_to(jnp.arange(SEQ, dtype=jnp.int32)[None, :], (BATCH, SEQ)).astype(jnp.int32)
    sc_w_in = nrm(ks[1], (N_CONV_LAYERS, d, 3 * d), d ** -0.5)
    sc_conv_w = nrm(ks[2], (N_CONV_LAYERS, SC_WIDTH, d), SC_WIDTH ** -0.5)
    sc_w_out = nrm(ks[3], (N_CONV_LAYERS, d, d), d ** -0.5 * DEEPNORM_BETA)
    mla_w_in = nrm(ks[4], (N_MLA_LAYERS, d, MLA_Q_LORA + MLA_KV_LORA + MLA_ROPE), d ** -0.5)
    mla_q_norm = 1.0 + nrm(ks[5], (N_MLA_LAYERS, MLA_Q_LORA), 0.01)
    mla_kv_norm = 1.0 + nrm(ks[6], (N_MLA_LAYERS, MLA_KV_LORA), 0.01)
    mla_w_uq = nrm(ks[7], (N_MLA_LAYERS, MLA_Q_LORA, MLA_HEADS * MLA_QK), MLA_Q_LORA ** -0.5)
    mla_w_ukv = nrm(ks[8], (N_MLA_LAYERS, MLA_KV_LORA, MLA_HEADS * (MLA_NOPE + MLA_V)), MLA_KV_LORA ** -0.5)
    mla_w_o = nrm(ks[9], (N_MLA_LAYERS, MLA_HEADS * MLA_V, d), (MLA_HEADS * MLA_V) ** -0.5 * DEEPNORM_BETA)
    ssm_w_in = nrm(ks[10], (N_SSM_LAYERS, d, SSM_IN_DIM), d ** -0.5)
    ssm_conv_w = nrm(ks[11], (N_SSM_LAYERS, SSM_CONV, SSM_CONV_DIM), SSM_CONV ** -0.5)
    ssm_conv_b = nrm(ks[12], (N_SSM_LAYERS, SSM_CONV_DIM), 0.01)
    u = jax.random.uniform(ks[13], (N_SSM_LAYERS, SSM_HEADS), dtype=jnp.float32)
    dt0 = jnp.exp(u * (math.log(0.1) - math.log(0.001)) + math.log(0.001))
    ssm_dt_bias = dt0 + jnp.log(-jnp.expm1(-dt0))
    ssm_a_log = jnp.log(jax.random.uniform(ks[14], (N_SSM_LAYERS, SSM_HEADS), dtype=jnp.float32, minval=1.0, maxval=16.0))
    ssm_d = 1.0 + nrm(ks[15], (N_SSM_LAYERS, SSM_HEADS), 0.01)
    ssm_norm_w = 1.0 + nrm(ks[16], (N_SSM_LAYERS, SSM_INNER), 0.01)
    ssm_w_out = nrm(ks[17], (N_SSM_LAYERS, SSM_INNER, d), SSM_INNER ** -0.5 * DEEPNORM_BETA)
    peer_w_q = nrm(ks[18], (DEPTH, d, PEER_HEADS * PEER_KEY_DIM), d ** -0.5)
    peer_sub_keys = nrm(ks[19], (DEPTH, PEER_HEADS, 2, PEER_N_KEYS, PEER_HALF), PEER_HALF ** -0.5)
    peer_u = nrm(ks[20], (DEPTH, PEER_EXPERTS, d), d ** -0.5)
    peer_v = nrm(ks[21], (DEPTH, PEER_EXPERTS, d), (PEER_HEADS * PEER_TOPK) ** -0.5 * DEEPNORM_BETA)
    ln_g = 1.0 + nrm(ks[22], (DEPTH, 2, d), 0.01)
    ln_b = nrm(ks[23], (DEPTH, 2, d), 0.01)
    return {'x': x, 'positions': positions,
            'sc_w_in': sc_w_in, 'sc_conv_w': sc_conv_w, 'sc_w_out': sc_w_out,
            'mla_w_in': mla_w_in, 'mla_q_norm': mla_q_norm, 'mla_kv_norm': mla_kv_norm,
            'mla_w_uq': mla_w_uq, 'mla_w_ukv': mla_w_ukv, 'mla_w_o': mla_w_o,
            'ssm_w_in': ssm_w_in, 'ssm_conv_w': ssm_conv_w, 'ssm_conv_b': ssm_conv_b,
            'ssm_dt_bias': ssm_dt_bias, 'ssm_a_log': ssm_a_log, 'ssm_d': ssm_d,
            'ssm_norm_w': ssm_norm_w, 'ssm_w_out': ssm_w_out,
            'peer_w_q': peer_w_q, 'peer_sub_keys': peer_sub_keys, 'peer_u': peer_u, 'peer_v': peer_v,
            'ln_g': ln_g, 'ln_b': ln_b}


def reference(x, positions, sc_w_in, sc_conv_w, sc_w_out, mla_w_in, mla_q_norm, mla_kv_norm,
              mla_w_uq, mla_w_ukv, mla_w_o, ssm_w_in, ssm_conv_w, ssm_conv_b, ssm_dt_bias,
              ssm_a_log, ssm_d, ssm_norm_w, ssm_w_out, peer_w_q, peer_sub_keys, peer_u, peer_v,
              ln_g, ln_b):
    for i in range(DEPTH):
        j = i // N_MIXERS
        kind = i % N_MIXERS
        if kind == 0:
            h = short_conv_mixer(x, sc_w_in[j], sc_conv_w[j], sc_w_out[j])
        elif kind == 1:
            h = mla_mixer(x, positions, mla_w_in[j], mla_q_norm[j], mla_kv_norm[j],
                          mla_w_uq[j], mla_w_ukv[j], mla_w_o[j])
        else:
            h = mamba2_mixer(x, ssm_w_in[j], ssm_conv_w[j], ssm_conv_b[j], ssm_dt_bias[j],
                             ssm_a_log[j], ssm_d[j], ssm_norm_w[j], ssm_w_out[j])
        x = layer_norm(DEEPNORM_ALPHA * x + h, ln_g[i, 0], ln_b[i, 0])
        f = peer_ffn(x, peer_w_q[i], peer_sub_keys[i], peer_u[i], peer_v[i])
        x = layer_norm(DEEPNORM_ALPHA * x + f, ln_g[i, 1], ln_b[i, 1])
    return x
```

```python
import functools
import math

import jax
import jax.numpy as jnp
from jax import lax
from jax.experimental import pallas as pl
from jax.experimental.pallas import tpu as pltpu

F32 = jnp.float32
BF16 = jnp.bfloat16

LN_EPS = 1e-5
RMS_EPS = 1e-6
N_MIXERS = 3

MLA_HEADS = 16
MLA_NOPE = 128
MLA_ROPE = 64
MLA_V = 128
MLA_QK = MLA_NOPE + MLA_ROPE
MLA_HEAD_PAD = 256
MLA_HEAD_GROUP = 4
ROPE_THETA = 10000.0

SSM_HEAD_DIM = 64
SSM_GROUPS = 8
SSM_STATE = 128
SSM_CONV = 4
SSM_CHUNK = 128

PEER_HEADS = 8
PEER_KEY_DIM = 256
PEER_HALF = 128
PEER_N_KEYS = 128
PEER_TOPK = 16
PEER_CAND_PER_A = tuple(PEER_TOPK // (a + 1) for a in range(PEER_TOPK))
PEER_N_CAND = sum(PEER_CAND_PER_A)
PEER_CAND_ROWS = -(-PEER_N_CAND // 8) * 8

VMEM_LIMIT = 56 * 1024 * 1024


def _params(*sem):
    return pltpu.CompilerParams(dimension_semantics=sem, vmem_limit_bytes=VMEM_LIMIT)


def _dot(a, b):
    return jnp.dot(a, b, preferred_element_type=F32)


def _dot_nt(a, b):
    return lax.dot_general(a, b, (((1,), (1,)), ((), ())), preferred_element_type=F32)


def _dot_exact(a, b):
    return jnp.dot(a, b, preferred_element_type=F32, precision=lax.Precision.HIGHEST)


def _mm_kernel(a_ref, w_ref, o_ref):
    o_ref[...] = _dot(a_ref[...], w_ref[...]).astype(o_ref.dtype)


def matmul(a, w, out_dtype, tm=1024, tn=512):
    m, k = a.shape
    n = w.shape[1]
    tm = min(tm, m)
    tn = min(tn, n)
    assert m % tm == 0 and n % tn == 0
    return pl.pallas_call(
        _mm_kernel,
        grid=(m // tm, n // tn),
        in_specs=[pl.BlockSpec((tm, k), lambda i, j: (i, 0)),
                  pl.BlockSpec((k, tn), lambda i, j: (0, j))],
        out_specs=pl.BlockSpec((tm, tn), lambda i, j: (i, j)),
        out_shape=jax.ShapeDtypeStruct((m, n), out_dtype),
        compiler_params=_params("parallel", "arbitrary"),
        name="matmul",
    )(a, w)


def _layer_norm_rows(y, g, b):
    mu = jnp.mean(y, axis=-1, keepdims=True)
    yc = y - mu
    var = jnp.mean(yc * yc, axis=-1, keepdims=True)
    return yc * lax.rsqrt(var + LN_EPS) * g + b


def _mm_res_ln_kernel(alpha, a_ref, w_ref, res_ref, g_ref, b_ref, o32_ref, o16_ref, acc_ref):
    kk = pl.program_id(1)

    @pl.when(kk == 0)
    def _():
        acc_ref[...] = jnp.zeros_like(acc_ref)

    acc_ref[...] += _dot(a_ref[...], w_ref[...])

    @pl.when(kk == pl.num_programs(1) - 1)
    def _():
        y = _layer_norm_rows(alpha * res_ref[...] + acc_ref[...], g_ref[...], b_ref[...])
        o32_ref[...] = y
        o16_ref[...] = y.astype(BF16)


def matmul_residual_ln(a, w, res, g, b, alpha, tm=512, tk=1024):
    m, k = a.shape
    d = w.shape[1]
    tm = min(tm, m)
    tk = min(tk, k)
    assert m % tm == 0 and k % tk == 0
    return pl.pallas_call(
        functools.partial(_mm_res_ln_kernel, alpha),
        grid=(m // tm, k // tk),
        in_specs=[pl.BlockSpec((tm, tk), lambda i, j: (i, j)),
                  pl.BlockSpec((tk, d), lambda i, j: (j, 0)),
                  pl.BlockSpec((tm, d), lambda i, j: (i, 0)),
                  pl.BlockSpec((1, d), lambda i, j: (0, 0)),
                  pl.BlockSpec((1, d), lambda i, j: (0, 0))],
        out_specs=[pl.BlockSpec((tm, d), lambda i, j: (i, 0)),
                   pl.BlockSpec((tm, d), lambda i, j: (i, 0))],
        out_shape=[jax.ShapeDtypeStruct((m, d), F32), jax.ShapeDtypeStruct((m, d), BF16)],
        scratch_shapes=[pltpu.VMEM((tm, d), F32)],
        compiler_params=_params("parallel", "arbitrary"),
        name="matmul_residual_ln",
    )(a, w, res, g.reshape(1, d), b.reshape(1, d))


def _res_ln_kernel(alpha, h_ref, res_ref, g_ref, b_ref, o32_ref, o16_ref):
    y = _layer_norm_rows(alpha * res_ref[...] + h_ref[...], g_ref[...], b_ref[...])
    o32_ref[...] = y
    o16_ref[...] = y.astype(BF16)


def residual_ln(h, res, g, b, alpha, tm=512):
    m, d = res.shape
    tm = min(tm, m)
    row = pl.BlockSpec((tm, d), lambda i: (i, 0))
    vec = pl.BlockSpec((1, d), lambda i: (0, 0))
    return pl.pallas_call(
        functools.partial(_res_ln_kernel, alpha),
        grid=(m // tm,),
        in_specs=[row, row, vec, vec],
        out_specs=[row, row],
        out_shape=[jax.ShapeDtypeStruct((m, d), F32), jax.ShapeDtypeStruct((m, d), BF16)],
        compiler_params=_params("parallel"),
        name="residual_ln",
    )(h, res, g.reshape(1, d), b.reshape(1, d))


def _causal_conv_rows(u, tail, w):
    n = u.shape[0]
    k = w.shape[0]
    row = lax.broadcasted_iota(jnp.int32, u.shape, 0)
    out = u * w[k - 1:k, :]
    for shift in range(1, k):
        shifted = pltpu.roll(u, shift, axis=0)
        for r in range(shift):
            shifted = jnp.where(row == r, tail[8 - shift + r:8 - shift + r + 1, :], shifted)
        out = out + shifted * w[k - 1 - shift:k - shift, :]
    return out


def _short_conv_kernel(gb_ref, gc_ref, h_ref, w_ref, y_ref, tail_ref):
    @pl.when(pl.program_id(1) == 0)
    def _():
        tail_ref[...] = jnp.zeros_like(tail_ref)

    u = gc_ref[...].astype(F32) * h_ref[...].astype(F32)
    conv = _causal_conv_rows(u, tail_ref[...], w_ref[...])
    y_ref[...] = (gb_ref[...].astype(F32) * conv).astype(y_ref.dtype)
    tail_ref[...] = u[u.shape[0] - 8:, :]


def short_conv_gate(proj, conv_w, batch, seq, ts=512):
    t, c3 = proj.shape
    c = c3 // 3
    ts = min(ts, seq)
    ns = seq // ts
    blk = lambda col: pl.BlockSpec((ts, c), lambda b, s, col=col: (b * ns + s, col))
    return pl.pallas_call(
        _short_conv_kernel,
        grid=(batch, ns),
        in_specs=[blk(0), blk(1), blk(2), pl.BlockSpec(conv_w.shape, lambda b, s: (0, 0))],
        out_specs=pl.BlockSpec((ts, c), lambda b, s: (b * ns + s, 0)),
        out_shape=jax.ShapeDtypeStruct((t, c), BF16),
        scratch_shapes=[pltpu.VMEM((8, c), F32)],
        compiler_params=_params("parallel", "arbitrary"),
        name="short_conv_gate",
    )(proj, proj, proj, conv_w)


def _rms_rows(y, w):
    return y * lax.rsqrt(jnp.mean(y * y, axis=-1, keepdims=True) + RMS_EPS) * w


def _mla_proj_kernel(scale, x_ref, pos_ref, invf_ref, wqa_ref, wkva_ref, wkr_ref, qn_ref, kvn_ref,
                     wqn_ref, wqr_ref, wkn_ref, wv_ref, q_ref, k_ref, v_ref,
                     cq_ref, ckv_ref, cs_ref, kr_ref):
    hg = MLA_HEAD_GROUP

    @pl.when(pl.program_id(1) == 0)
    def _():
        x = x_ref[...]
        cq_ref[...] = _rms_rows(_dot(x, wqa_ref[...]), qn_ref[...]).astype(BF16)
        ckv_ref[...] = _rms_rows(_dot(x, wkva_ref[...]), kvn_ref[...]).astype(BF16)
        ang = pos_ref[...].astype(F32) * invf_ref[...]
        lane = lax.broadcasted_iota(jnp.int32, ang.shape, 1)
        cs = jnp.where(lane < MLA_ROPE, jnp.cos(ang), jnp.sin(ang))
        cs_ref[...] = cs
        m = _dot(x, wkr_ref[...]) * cs
        kr = m + pltpu.roll(m, MLA_ROPE, axis=1)
        kr_ref[...] = jnp.where(lane < MLA_ROPE, kr, 0.0).astype(BF16)

    cq = cq_ref[...]
    ckv = ckv_ref[...]
    cs = cs_ref[...]
    kr = kr_ref[...]
    qn = _dot(cq, wqn_ref[...]) * scale
    qr = _dot(cq, wqr_ref[...])
    kn = _dot(ckv, wkn_ref[...])
    v_ref[...] = _dot(ckv, wv_ref[...]).astype(BF16)
    for h in range(hg):
        lo = h * MLA_HEAD_PAD
        sl = slice(h * 128, (h + 1) * 128)
        m = qr[:, sl] * cs
        rope = (m + pltpu.roll(m, MLA_ROPE, axis=1)) * scale
        q_ref[:, lo:lo + 128] = qn[:, sl].astype(BF16)
        q_ref[:, lo + 128:lo + 256] = rope.astype(BF16)
        k_ref[:, lo:lo + 128] = kn[:, sl].astype(BF16)
        k_ref[:, lo + 128:lo + 256] = kr


def mla_project(xb, pos, w, tm=512):
    t, d = xb.shape
    tm = min(tm, t)
    hg = MLA_HEAD_GROUP
    ng = MLA_HEADS // hg
    scale = 1.0 / math.sqrt(MLA_QK)
    full = lambda a: pl.BlockSpec(a.shape, lambda i, j: (0, 0))
    colblk = lambda a, wd: pl.BlockSpec((a.shape[0], wd), lambda i, j: (0, j))
    out_w = hg * MLA_HEAD_PAD
    return pl.pallas_call(
        functools.partial(_mla_proj_kernel, scale),
        grid=(t // tm, ng),
        in_specs=[pl.BlockSpec((tm, d), lambda i, j: (i, 0)),
                  pl.BlockSpec((tm, 1), lambda i, j: (i, 0)),
                  full(w["invf"]), full(w["wqa"]), full(w["wkva"]), full(w["wkr"]),
                  full(w["qn"]), full(w["kvn"]),
                  colblk(w["wqn"], hg * 128), colblk(w["wqr"], hg * 128),
                  colblk(w["wkn"], hg * 128), colblk(w["wv"], hg * 128)],
        out_specs=[pl.BlockSpec((tm, out_w), lambda i, j: (i, j)),
                   pl.BlockSpec((tm, out_w), lambda i, j: (i, j)),
                   pl.BlockSpec((tm, hg * MLA_V), lambda i, j: (i, j))],
        out_shape=[jax.ShapeDtypeStruct((t, MLA_HEADS * MLA_HEAD_PAD), BF16),
                   jax.ShapeDtypeStruct((t, MLA_HEADS * MLA_HEAD_PAD), BF16),
                   jax.ShapeDtypeStruct((t, MLA_HEADS * MLA_V), BF16)],
        scratch_shapes=[pltpu.VMEM((tm, w["wqa"].shape[1]), BF16),
                        pltpu.VMEM((tm, w["wkva"].shape[1]), BF16),
                        pltpu.VMEM((tm, 128), F32),
                        pltpu.VMEM((tm, 128), BF16)],
        compiler_params=_params("parallel", "arbitrary"),
        name="mla_project",
    )(xb, pos, w["invf"], w["wqa"], w["wkva"], w["wkr"], w["qn"], w["kvn"],
      w["wqn"], w["wqr"], w["wkn"], w["wv"])


def _attn_kernel(tq, q_ref, k_ref, v_ref, o_ref):
    qi = pl.program_id(2)
    q = q_ref[...]

    def block(j, carry, masked):
        m_prev, l_prev, acc = carry
        start = pl.multiple_of(j * tq, tq)
        k = k_ref[pl.ds(start, tq), :]
        v = v_ref[pl.ds(start, tq), :]
        s = _dot_nt(q, k)
        if masked:
            r = lax.broadcasted_iota(jnp.int32, s.shape, 0)
            c = lax.broadcasted_iota(jnp.int32, s.shape, 1)
            s = jnp.where(r >= c, s, -jnp.inf)
        m_new = jnp.maximum(m_prev, jnp.max(s, axis=-1, keepdims=True))
        p = jnp.exp(s - m_new)
        corr = jnp.exp(m_prev - m_new)
        l_new = corr * l_prev + jnp.sum(p, axis=-1, keepdims=True)
        acc = corr * acc + _dot(p.astype(BF16), v)
        return m_new, l_new, acc

    init = (jnp.full((tq, 1), -jnp.inf, F32), jnp.zeros((tq, 1), F32),
            jnp.zeros((tq, v_ref.shape[1]), F32))
    carry = lax.fori_loop(0, qi, lambda j, c: block(j, c, False), init)
    _, l, acc = block(qi, carry, True)
    o_ref[...] = (acc / l).astype(o_ref.dtype)


def mla_attention(q, k, v, batch, seq, tq=512):
    t = q.shape[0]
    tq = min(tq, seq)
    nq = seq // tq
    return pl.pallas_call(
        functools.partial(_attn_kernel, tq),
        grid=(batch, MLA_HEADS, nq),
        in_specs=[pl.BlockSpec((tq, MLA_HEAD_PAD), lambda b, h, i: (b * nq + i, h)),
                  pl.BlockSpec((seq, MLA_HEAD_PAD), lambda b, h, i: (b, h)),
                  pl.BlockSpec((seq, MLA_V), lambda b, h, i: (b, h))],
        out_specs=pl.BlockSpec((tq, MLA_V), lambda b, h, i: (b * nq + i, h)),
        out_shape=jax.ShapeDtypeStruct((t, MLA_HEADS * MLA_V), BF16),
        compiler_params=_params("parallel", "parallel", "arbitrary"),
        name="mla_attention",
    )(q, k, v)


def _rotate_half_cols(w):
    half = w.shape[-1] // 2
    return jnp.concatenate([-w[..., half:], w[..., :half]], axis=-1)


def mla_weights(w_in, q_norm, kv_norm, w_uq, w_ukv):
    ql = q_norm.shape[0]
    kvl = kv_norm.shape[0]
    w_kr = w_in[:, ql + kvl:]
    uq = w_uq.reshape(ql, MLA_HEADS, MLA_QK)
    uq_r = uq[..., MLA_NOPE:]
    ukv = w_ukv.reshape(kvl, MLA_HEADS, MLA_NOPE + MLA_V)
    inv_freq = 1.0 / (ROPE_THETA ** (jnp.arange(0, MLA_ROPE, 2, dtype=F32) / MLA_ROPE))
    invf = jnp.concatenate([inv_freq] * 4).reshape(1, 128)
    return {
        "invf": invf,
        "wqa": w_in[:, :ql].astype(BF16),
        "wkva": w_in[:, ql:ql + kvl].astype(BF16),
        "wkr": jnp.concatenate([w_kr, _rotate_half_cols(w_kr)], axis=-1).astype(BF16),
        "qn": q_norm.reshape(1, ql),
        "kvn": kv_norm.reshape(1, kvl),
        "wqn": uq[..., :MLA_NOPE].reshape(ql, MLA_HEADS * MLA_NOPE).astype(BF16),
        "wqr": jnp.concatenate([uq_r, _rotate_half_cols(uq_r)], axis=-1)
                  .reshape(ql, MLA_HEADS * 2 * MLA_ROPE).astype(BF16),
        "wkn": ukv[..., :MLA_NOPE].reshape(kvl, MLA_HEADS * MLA_NOPE).astype(BF16),
        "wv": ukv[..., MLA_NOPE:].reshape(kvl, MLA_HEADS * MLA_V).astype(BF16),
    }


def _ssd_kernel(z_ref, xbc_ref, dt_ref, cw_ref, cb_ref, dtb_ref, alog_ref, dskip_ref, nw_ref,
                expand_ref, y_ref, tail_ref, state_ref):
    q = SSM_CHUNK
    n = SSM_STATE
    heads = dt_ref.shape[1]
    inner = heads * SSM_HEAD_DIM
    gw = inner // SSM_GROUPS
    pairs_per_group = gw // 128

    @pl.when(pl.program_id(1) == 0)
    def _():
        tail_ref[...] = jnp.zeros_like(tail_ref)
        state_ref[...] = jnp.zeros_like(state_ref)

    raw = xbc_ref[...].astype(F32)
    conv = _causal_conv_rows(raw, tail_ref[...], cw_ref[...]) + cb_ref[...]
    tail_ref[...] = raw[q - 8:, :]
    xbc = conv * jax.nn.sigmoid(conv)
    xs = xbc[:, :inner]

    dt = jax.nn.softplus(dt_ref[...] + dtb_ref[...])
    da = dt * (-jnp.exp(alog_ref[...]))
    r = lax.broadcasted_iota(jnp.int32, (q, q), 0)
    c = lax.broadcasted_iota(jnp.int32, (q, q), 1)
    causal = r >= c
    acum = _dot_exact(causal.astype(F32), da)
    acum_t = jnp.concatenate([acum, jnp.zeros((q, 128 - heads), F32)], axis=1).T
    last = acum[q - 1:q, :]
    stacked = jnp.concatenate(
        [dt, jnp.exp(acum), jnp.exp(last - acum), jnp.broadcast_to(jnp.exp(last), (8, heads))], axis=0)
    ex = _dot_exact(stacked, expand_ref[...])
    xdt = xs * ex[:q]
    grow = ex[q:2 * q]
    xdt_dec = (xdt * ex[2 * q:3 * q]).astype(BF16)
    state_decay = ex[3 * q:3 * q + 1]
    xdt = xdt.astype(BF16)
    lane = lax.broadcasted_iota(jnp.int32, (q, 128), 1)
    first = lane < SSM_HEAD_DIM

    for g in range(SSM_GROUPS):
        bm = xbc[:, inner + g * n:inner + (g + 1) * n]
        cm = xbc[:, inner + SSM_GROUPS * n + g * n:inner + SSM_GROUPS * n + (g + 1) * n].astype(BF16)
        bm_t = bm.T.astype(BF16)
        cb = _dot(cm, bm_t)
        ys = []
        for pr in range(pairs_per_group):
            pair = g * pairs_per_group + pr
            sl = slice(pair * 128, (pair + 1) * 128)
            yd = []
            for e in range(2):
                hd = 2 * pair + e
                seg = acum[:, hd:hd + 1] - acum_t[hd:hd + 1, :]
                decay = jnp.exp(jnp.where(causal, seg, -jnp.inf))
                yd.append(_dot((cb * decay).astype(BF16), xdt[:, sl]))
            y_diag = jnp.where(first, yd[0], yd[1])
            st = state_ref[pair]
            y_off = _dot(cm, st.astype(BF16)) * grow[:, sl]
            state_ref[pair] = st * state_decay[:, sl] + _dot(bm_t, xdt_dec[:, sl])
            ys.append(y_diag + y_off)
        gsl = slice(g * gw, (g + 1) * gw)
        zg = z_ref[:, gsl].astype(F32)
        yg = jnp.concatenate(ys, axis=1) + dskip_ref[:, gsl] * xs[:, gsl]
        yg = yg * (zg * jax.nn.sigmoid(zg))
        yg = yg * lax.rsqrt(jnp.mean(yg * yg, axis=-1, keepdims=True) + RMS_EPS)
        y_ref[:, gsl] = (yg * nw_ref[:, gsl]).astype(y_ref.dtype)


def ssd_mixer_core(z, xbc, dt_raw, conv_w, conv_b, dt_bias, a_log, d_skip, norm_w, batch, seq):
    t, inner = z.shape
    cdim = xbc.shape[1]
    heads = dt_raw.shape[1]
    nc = seq // SSM_CHUNK
    q = SSM_CHUNK
    expand = jnp.repeat(jnp.eye(heads, dtype=F32), SSM_HEAD_DIM, axis=1)
    row = lambda wd: pl.BlockSpec((q, wd), lambda b, s: (b * nc + s, 0))
    full = lambda a: pl.BlockSpec(a.shape, lambda b, s: (0, 0))
    args = [conv_w, conv_b.reshape(1, cdim), dt_bias.reshape(1, heads), a_log.reshape(1, heads),
            jnp.repeat(d_skip, SSM_HEAD_DIM).reshape(1, inner), norm_w.reshape(1, inner), expand]
    return pl.pallas_call(
        _ssd_kernel,
        grid=(batch, nc),
        in_specs=[row(inner), row(cdim), row(heads)] + [full(a) for a in args],
        out_specs=row(inner),
        out_shape=jax.ShapeDtypeStruct((t, inner), BF16),
        scratch_shapes=[pltpu.VMEM((8, cdim), F32),
                        pltpu.VMEM((inner // 128, SSM_STATE, 128), F32)],
        compiler_params=_params("parallel", "arbitrary"),
        name="ssd_mixer_core",
    )(z, xbc, dt_raw, *args)


def _extract_topk(cur, n_take, vals_ref=None):
    rows = cur.shape[0]
    row = lax.broadcasted_iota(jnp.int32, cur.shape, 0)
    rank = jnp.full(cur.shape, float(n_take), F32)
    for k in range(n_take):
        m = jnp.max(cur, axis=0, keepdims=True)
        first = jnp.min(jnp.where(cur == m, row, rows), axis=0, keepdims=True)
        sel = row == first
        rank = jnp.where(sel, float(k), rank)
        cur = jnp.where(sel, -jnp.inf, cur)
        if vals_ref is not None:
            vals_ref[k:k + 1, :] = m
    return rank


def _peer_route_kernel(x_ref, wq_ref, keys_ref, cnt_ref, e1_ref, rank2_ref, e2_ref,
                       q_ref, v1_ref, v2_ref, cand_ref):
    @pl.when(pl.program_id(1) == 0)
    def _():
        q_ref[...] = _dot(x_ref[...], wq_ref[...]).astype(BF16)

    h = pl.program_id(1)
    off = pl.multiple_of(h * PEER_KEY_DIM, PEER_KEY_DIM)
    q1 = q_ref[:, pl.ds(off, PEER_HALF)]
    q2 = q_ref[:, pl.ds(off + PEER_HALF, PEER_HALF)]
    s1 = _dot_nt(keys_ref[0, 0], q1)
    s2 = _dot_nt(keys_ref[0, 1], q2)
    rank1 = _extract_topk(s1, PEER_TOPK, v1_ref)
    rank2 = _extract_topk(s2, PEER_TOPK, v2_ref)

    off_rows = 0
    for a, nb in enumerate(PEER_CAND_PER_A):
        cand_ref[off_rows:off_rows + nb, :] = v1_ref[a:a + 1, :] + v2_ref[0:nb, :]
        off_rows += nb
    if PEER_CAND_ROWS > PEER_N_CAND:
        cand_ref[PEER_N_CAND:, :] = jnp.full((PEER_CAND_ROWS - PEER_N_CAND, cand_ref.shape[1]), -jnp.inf, F32)
    cand = cand_ref[...]
    taken = _extract_topk(cand, PEER_TOPK) < float(PEER_TOPK)
    top = v1_ref[0:1, :] + v2_ref[0:1, :]
    zsum = jnp.sum(jnp.where(taken, jnp.exp(cand - top), 0.0), axis=0, keepdims=True)
    takenf = taken.astype(F32)

    cnt_key = jnp.zeros(rank1.shape, F32)
    off_rows = 0
    for a, nb in enumerate(PEER_CAND_PER_A):
        cnt_a = jnp.sum(takenf[off_rows:off_rows + nb, :], axis=0, keepdims=True)
        cnt_key = jnp.where(rank1 == float(a), cnt_a, cnt_key)
        off_rows += nb

    cnt_ref[0] = cnt_key
    e1_ref[0] = jnp.exp(s1 - v1_ref[0:1, :])
    rank2_ref[0] = rank2
    e2_ref[0] = jnp.exp(s2 - v2_ref[0:1, :]) / zsum


def peer_route(xb, wq, keys, tm=256):
    t, d = xb.shape
    tm = min(tm, t)
    out = jax.ShapeDtypeStruct((PEER_HEADS, PEER_N_KEYS, t), F32)
    oblk = pl.BlockSpec((1, PEER_N_KEYS, tm), lambda i, h: (h, 0, i))
    return pl.pallas_call(
        _peer_route_kernel,
        grid=(t // tm, PEER_HEADS),
        in_specs=[pl.BlockSpec((tm, d), lambda i, h: (i, 0)),
                  pl.BlockSpec(wq.shape, lambda i, h: (0, 0)),
                  pl.BlockSpec((1, 2, PEER_N_KEYS, PEER_HALF), lambda i, h: (h, 0, 0, 0))],
        out_specs=[oblk, oblk, oblk, oblk],
        out_shape=[out, out, out, out],
        scratch_shapes=[pltpu.VMEM((tm, wq.shape[1]), BF16),
                        pltpu.VMEM((PEER_TOPK, tm), F32),
                        pltpu.VMEM((PEER_TOPK, tm), F32),
                        pltpu.VMEM((PEER_CAND_ROWS, tm), F32)],
        compiler_params=_params("parallel", "arbitrary"),
        name="peer_route",
    )(xb, wq, keys)


def _gelu_exact(h):
    return 0.5 * h * (1.0 + lax.erf(h * (1.0 / math.sqrt(2.0))))


def _peer_expert_kernel(n_i1, x_ref, u_ref, v_ref, cnt_ref, e1_ref, rank2_ref, e2_ref, o_ref, hid_ref):
    j = pl.program_id(1)

    @pl.when(j == 0)
    def _():
        o_ref[...] = jnp.zeros_like(o_ref)

    x = x_ref[...]
    for ii in range(n_i1):
        i1 = j * n_i1 + ii
        h_t = _dot_nt(u_ref[ii * PEER_N_KEYS:(ii + 1) * PEER_N_KEYS, :], x)
        gate = jnp.zeros(h_t.shape, F32)
        for hd in range(PEER_HEADS):
            cnt = cnt_ref[hd, pl.ds(i1, 1), :]
            e1 = e1_ref[hd, pl.ds(i1, 1), :]
            gate = gate + jnp.where(rank2_ref[hd] < cnt, e2_ref[hd], 0.0) * e1
        hid_t = _gelu_exact(h_t) * gate
        hid_ref[:, ii * PEER_N_KEYS:(ii + 1) * PEER_N_KEYS] = hid_t.T.astype(BF16)
    o_ref[...] += _dot(hid_ref[...], v_ref[...])


def peer_experts(xb, u_tab, v_tab, cnt, e1, rank2, e2, tm=512, te=512):
    t, d = xb.shape
    n_exp = u_tab.shape[0]
    tm = min(tm, t)
    n_i1 = te // PEER_N_KEYS
    gblk = pl.BlockSpec((PEER_HEADS, PEER_N_KEYS, tm), lambda i, j: (0, 0, i))
    return pl.pallas_call(
        functools.partial(_peer_expert_kernel, n_i1),
        grid=(t // tm, n_exp // te),
        in_specs=[pl.BlockSpec((tm, d), lambda i, j: (i, 0)),
                  pl.BlockSpec((te, d), lambda i, j: (j, 0)),
                  pl.BlockSpec((te, d), lambda i, j: (j, 0)),
                  gblk, gblk, gblk, gblk],
        out_specs=pl.BlockSpec((tm, d), lambda i, j: (i, 0)),
        out_shape=jax.ShapeDtypeStruct((t, d), F32),
        scratch_shapes=[pltpu.VMEM((tm, te), BF16)],
        compiler_params=_params("parallel", "arbitrary"),
        name="peer_experts",
    )(xb, u_tab, v_tab, cnt, e1, rank2, e2)


def peer_ffn(xb, w_q, sub_keys, u_tab, v_tab):
    cnt, e1, rank2, e2 = peer_route(xb, w_q.astype(BF16), sub_keys.astype(BF16))
    return peer_experts(xb, u_tab.astype(BF16), v_tab.astype(BF16), cnt, e1, rank2, e2)


def short_conv_mixer(xb, w_in, conv_w, w_out, batch, seq):
    proj = matmul(xb, w_in.astype(BF16), BF16)
    return short_conv_gate(proj, conv_w, batch, seq), w_out.astype(BF16)


def mla_mixer(xb, pos, w_in, q_norm, kv_norm, w_uq, w_ukv, w_o, batch, seq):
    q, k, v = mla_project(xb, pos, mla_weights(w_in, q_norm, kv_norm, w_uq, w_ukv))
    return mla_attention(q, k, v, batch, seq), w_o.astype(BF16)


def mamba2_mixer(xb, w_in, conv_w, conv_b, dt_bias, a_log, d_skip, norm_w, w_out, batch, seq):
    inner = norm_w.shape[0]
    cdim = conv_b.shape[0]
    z = matmul(xb, w_in[:, :inner].astype(BF16), BF16)
    xbc = matmul(xb, w_in[:, inner:inner + cdim].astype(BF16), BF16)
    dt_raw = matmul(xb, w_in[:, inner + cdim:].astype(BF16), F32)
    y = ssd_mixer_core(z, xbc, dt_raw, conv_w, conv_b, dt_bias, a_log, d_skip, norm_w, batch, seq)
    return y, w_out.astype(BF16)


def kernel(x, positions, sc_w_in, sc_conv_w, sc_w_out, mla_w_in, mla_q_norm, mla_kv_norm, mla_w_uq,
           mla_w_ukv, mla_w_o, ssm_w_in, ssm_conv_w, ssm_conv_b, ssm_dt_bias, ssm_a_log, ssm_d,
           ssm_norm_w, ssm_w_out, peer_w_q, peer_sub_keys, peer_u, peer_v, ln_g, ln_b):
    batch, seq, d = x.shape
    depth = peer_w_q.shape[0]
    alpha = (2 * depth) ** 0.25
    x32 = x.reshape(batch * seq, d)
    xb = x32.astype(BF16)
    pos = positions.reshape(batch * seq, 1)
    for i in range(depth):
        j = i // N_MIXERS
        kind = i % N_MIXERS
        if kind == 0:
            y, w_out = short_conv_mixer(xb, sc_w_in[j], sc_conv_w[j], sc_w_out[j], batch, seq)
        elif kind == 1:
            y, w_out = mla_mixer(xb, pos, mla_w_in[j], mla_q_norm[j], mla_kv_norm[j], mla_w_uq[j],
                                 mla_w_ukv[j], mla_w_o[j], batch, seq)
        else:
            y, w_out = mamba2_mixer(xb, ssm_w_in[j], ssm_conv_w[j], ssm_conv_b[j], ssm_dt_bias[j],
                                    ssm_a_log[j], ssm_d[j], ssm_norm_w[j], ssm_w_out[j], batch, seq)
        x32, xb = matmul_residual_ln(y, w_out, x32, ln_g[i, 0], ln_b[i, 0], alpha)
        f = peer_ffn(xb, peer_w_q[i], peer_sub_keys[i], peer_u[i], peer_v[i])
        x32, xb = residual_ln(f, x32, ln_g[i, 1], ln_b[i, 1], alpha)
    return x32.reshape(batch, seq, d)
```
